```python
import math
import jax, jax.numpy as jnp
from jax import lax
import numpy as np

D_MODEL = 2048
BATCH = 4
SEQ = 2048
DEPTH = 2

HEAD_DIM = 128
DIL_GROUPS = ((128, 1), (512, 4), (2048, 16))
DIL_HEADS_PER_GROUP = 4
DIL_HEADS = DIL_HEADS_PER_GROUP * len(DIL_GROUPS)
FOX_HEADS = 8
N_HEADS = DIL_HEADS + FOX_HEADS
N_BRANCHES = 2
QKV_COLS = 3 * N_HEADS * HEAD_DIM
IN_COLS = QKV_COLS + FOX_HEADS + N_BRANCHES * D_MODEL
A_OUT = DIL_HEADS_PER_GROUP * HEAD_DIM
B_OUT = FOX_HEADS * HEAD_DIM
BLOCK = 128
ROPE_THETA = 10000.0
NEG_INF = -1e30
RMS_EPS = 1e-6
PEER_HEADS = 8
N_KEYS = 128
N_EXPERTS = N_KEYS * N_KEYS
PEER_KEY_HALF = 128
PEER_TOPK = 16
PEER_CHUNK = 128

kernel_name = "hybrid_dilated_fox_peer_block"


def rms_norm(x, gain):
    xf = x.astype(jnp.float32)
    y = xf * lax.rsqrt(jnp.mean(xf * xf, axis=-1, keepdims=True) + RMS_EPS)
    return (y * gain).astype(x.dtype)


def rotary(x, positions):
    half = HEAD_DIM // 2
    inv_freq = ROPE_THETA ** (-jnp.arange(half, dtype=jnp.float32) / half)
    ang = positions.astype(jnp.float32)[:, None] * inv_freq[None, :]
    cos = jnp.cos(ang)[None, :, None, :]
    sin = jnp.sin(ang)[None, :, None, :]
    xf = x.astype(jnp.float32)
    x1, x2 = xf[..., :half], xf[..., half:]
    return jnp.concatenate([x1 * cos - x2 * sin, x2 * cos + x1 * sin], axis=-1).astype(x.dtype)


def dilated_window_attention(q, k, v, dilation, n_back):
    b, s, h, d = q.shape
    s_sub = s // dilation
    n_pad = (-s_sub) % BLOCK
    nb = (s_sub + n_pad) // BLOCK

    def to_sub(t):
        t = t.reshape(b, s_sub, dilation, h, d).transpose(0, 2, 3, 1, 4)
        t = jnp.pad(t, ((0, 0), (0, 0), (0, 0), (0, n_pad), (0, 0)))
        return t.reshape(b, dilation, h, nb, BLOCK, d)

    def with_prev(t):
        prev = jnp.pad(t, ((0, 0), (0, 0), (0, 0), (1, 0), (0, 0), (0, 0)))[:, :, :, :-1]
        return jnp.concatenate([prev, t], axis=4)

    qs = to_sub(q)
    kk = with_prev(to_sub(k))
    vv = with_prev(to_sub(v))
    scores = jnp.einsum('brhnqd,brhnkd->brhnqk', qs, kk).astype(jnp.float32) * (d ** -0.5)
    blk = jnp.arange(nb)[:, None]
    q_pos = blk * BLOCK + jnp.arange(BLOCK)[None, :]
    k_pos = (blk - 1) * BLOCK + jnp.arange(2 * BLOCK)[None, :]
    dist = q_pos[:, :, None] - k_pos[:, None, :]
    valid = (dist >= 0) & (dist <= n_back) & (k_pos[:, None, :] >= 0)
    scores = jnp.where(valid, scores, NEG_INF)
    lse = jax.nn.logsumexp(scores, axis=-1)
    probs = jnp.exp(scores - lse[..., None]).astype(v.dtype)
    out = jnp.einsum('brhnqk,brhnkd->brhnqd', probs, vv)

    def from_sub(t):
        t = t.reshape((b, dilation, h, nb * BLOCK) + t.shape[5:])[:, :, :, :s_sub]
        t = jnp.moveaxis(t, 3, 1)
        return t.reshape((b, s, h) + t.shape[4:])

    return from_sub(out), from_sub(lse)


def forgetting_attention(q, k, v, log_f):
    b, s, h, d = q.shape
    nb = s // BLOCK
    decay = jnp.cumsum(log_f, axis=1)
    k_decay = decay.transpose(0, 2, 1)
    k_pos = jnp.arange(s)
    qb = q.reshape(b, nb, BLOCK, h, d).transpose(1, 0, 2, 3, 4)
    db = decay.reshape(b, nb, BLOCK, h).transpose(1, 0, 3, 2)

    def one_block(args):
        q_blk, dec_blk, n = args
        q_pos = n * BLOCK + jnp.arange(BLOCK)
        scores = jnp.einsum('bqhd,bkhd->bhqk', q_blk, k).astype(jnp.float32) * (d ** -0.5)
        scores = scores + dec_blk[..., None] - k_decay[:, :, None, :]
        scores = jnp.where(k_pos[None, :] <= q_pos[:, None], scores, NEG_INF)
        probs = jax.nn.softmax(scores, axis=-1).astype(v.dtype)
        return jnp.einsum('bhqk,bkhd->bqhd', probs, v)

    out = lax.map(one_block, (qb, db, jnp.arange(nb)))
    return out.transpose(1, 0, 2, 3, 4).reshape(b, s, h, d)


def peer_ffn(xn, w_query, sub_keys, expert_u, expert_v):
    b, s, dm = xn.shape
    t = b * s
    x_tok = xn.reshape(t, dm)
    query = (x_tok @ w_query).reshape(t, PEER_HEADS, 2, PEER_KEY_HALF)
    half_scores = jnp.einsum('thcd,hcnd->thcn', query, sub_keys)
    top_s, top_i = lax.top_k(half_scores, PEER_TOPK)
    cand_s = (top_s[:, :, 0, :, None] + top_s[:, :, 1, None, :]).reshape(t, PEER_HEADS, PEER_TOPK * PEER_TOPK)
    cand_i = (top_i[:, :, 0, :, None] * N_KEYS + top_i[:, :, 1, None, :]).reshape(t, PEER_HEADS, PEER_TOPK * PEER_TOPK)
    best_s, best_pos = lax.top_k(cand_s, PEER_TOPK)
    expert_idx = jnp.take_along_axis(cand_i, best_pos, axis=-1)
    gates = jax.nn.softmax(best_s.astype(jnp.float32), axis=-1).astype(xn.dtype)
    n_chunks = t // PEER_CHUNK

    def one_chunk(args):
        xc, ic, gc = args
        u = expert_u[ic]
        v = expert_v[ic]
        act = jax.nn.gelu(jnp.einsum('cd,chkd->chk', xc, u), approximate=False)
        return jnp.einsum('chk,chkd->cd', gc * act, v)

    out = lax.map(one_chunk, (x_tok.reshape(n_chunks, PEER_CHUNK, dm),
                              expert_idx.reshape(n_chunks, PEER_CHUNK, PEER_HEADS, PEER_TOPK),
                              gates.reshape(n_chunks, PEER_CHUNK, PEER_HEADS, PEER_TOPK)))
    return out.reshape(b, s, dm)


def setup_inputs(seed: int = 0) -> dict:
    key = jax.random.key(seed)
    ks = jax.random.split(key, 15)
    f32 = jnp.float32
    nrm = lambda k, shape: jax.random.normal(k, shape, dtype=f32)
    return {
        'x': nrm(ks[0], (BATCH, SEQ, D_MODEL)),
        'mix_norm': 1.0 + 0.1 * nrm(ks[1], (DEPTH, D_MODEL)),
        'w_in': nrm(ks[2], (DEPTH, D_MODEL, IN_COLS)) * D_MODEL ** -0.5,
        'b_forget': 2.0 + 0.5 * nrm(ks[3], (DEPTH, FOX_HEADS)),
        'b_gate': 0.1 * nrm(ks[4], (DEPTH, N_BRANCHES, D_MODEL)),
        'q_norm': 1.0 + 0.1 * nrm(ks[5], (DEPTH, N_HEADS, HEAD_DIM)),
        'k_norm': 1.0 + 0.1 * nrm(ks[6], (DEPTH, N_HEADS, HEAD_DIM)),
        'w_branch_a': nrm(ks[7], (DEPTH, A_OUT, D_MODEL)) * A_OUT ** -0.5,
        'w_branch_b': nrm(ks[8], (DEPTH, B_OUT, D_MODEL)) * B_OUT ** -0.5,
        'w_out': nrm(ks[9], (DEPTH, D_MODEL, D_MODEL)) * D_MODEL ** -0.5,
        'ffn_norm': 1.0 + 0.1 * nrm(ks[10], (DEPTH, D_MODEL)),
        'w_query': nrm(ks[11], (DEPTH, D_MODEL, PEER_HEADS * 2 * PEER_KEY_HALF)) * D_MODEL ** -0.5,
        'sub_keys': nrm(ks[12], (DEPTH, PEER_HEADS, 2, N_KEYS, PEER_KEY_HALF)) * PEER_KEY_HALF ** -0.5,
        'expert_u': nrm(ks[13], (DEPTH, N_EXPERTS, D_MODEL)) * D_MODEL ** -0.5,
        'expert_v': nrm(ks[14], (DEPTH, N_EXPERTS, D_MODEL)) * PEER_HEADS ** -0.5,
    }


def reference(x, mix_norm, w_in, b_forget, b_gate, q_norm, k_norm, w_branch_a, w_branch_b,
              w_out, ffn_norm, w_query, sub_keys, expert_u, expert_v):
    b, s, _ = x.shape
    positions = jnp.arange(s)
    for l in range(DEPTH):
        xn = rms_norm(x, mix_norm[l])
        proj = xn @ w_in[l]
        qkv = proj[..., :QKV_COLS].reshape(b, s, 3, N_HEADS, HEAD_DIM)
        f_logit = proj[..., QKV_COLS:QKV_COLS + FOX_HEADS] + b_forget[l]
        gate = jax.nn.sigmoid(proj[..., QKV_COLS + FOX_HEADS:].reshape(b, s, N_BRANCHES, D_MODEL) + b_gate[l])
        q = rms_norm(qkv[:, :, 0], q_norm[l])
        k = rms_norm(qkv[:, :, 1], k_norm[l])
        v = qkv[:, :, 2]

        qa = rotary(q[:, :, :DIL_HEADS], positions)
        ka = rotary(k[:, :, :DIL_HEADS], positions)
        outs, lses = [], []
        for gi, (window, dilation) in enumerate(DIL_GROUPS):
            sl = slice(gi * DIL_HEADS_PER_GROUP, (gi + 1) * DIL_HEADS_PER_GROUP)
            o_g, lse_g = dilated_window_attention(qa[:, :, sl], ka[:, :, sl], v[:, :, sl],
                                                  dilation, window // dilation)
            outs.append(o_g)
            lses.append(lse_g)
        weights = jax.nn.softmax(jnp.stack(lses, axis=0), axis=0)
        o_a = jnp.sum(weights[..., None].astype(v.dtype) * jnp.stack(outs, axis=0), axis=0)
        o_a = o_a.reshape(b, s, A_OUT)

        log_f = jax.nn.log_sigmoid(f_logit.astype(jnp.float32))
        o_b = forgetting_attention(q[:, :, DIL_HEADS:], k[:, :, DIL_HEADS:], v[:, :, DIL_HEADS:], log_f)
        o_b = o_b.reshape(b, s, B_OUT)

        merged = gate[:, :, 0] * (o_a @ w_branch_a[l]) + gate[:, :, 1] * (o_b @ w_branch_b[l])
        x = x + merged @ w_out[l]

        x = x + peer_ffn(rms_norm(x, ffn_norm[l]), w_query[l], sub_keys[l], expert_u[l], expert_v[l])
    return x
```

```python
import functools

import jax
import jax.numpy as jnp
from jax import lax
from jax.experimental import pallas as pl
from jax.experimental.pallas import tpu as pltpu

D_MODEL = 2048
HEAD_DIM = 128
DIL_GROUPS = ((128, 1), (512, 4), (2048, 16))
DIL_HEADS_PER_GROUP = 4
DIL_HEADS = DIL_HEADS_PER_GROUP * len(DIL_GROUPS)
FOX_HEADS = 8
N_HEADS = DIL_HEADS + FOX_HEADS
QKV_COLS = 3 * N_HEADS * HEAD_DIM
A_OUT = DIL_HEADS_PER_GROUP * HEAD_DIM
B_OUT = FOX_HEADS * HEAD_DIM
BLOCK = 128
ROPE_THETA = 10000.0
NEG_INF = -1e30
RMS_EPS = 1e-6
PEER_HEADS = 8
N_KEYS = 128
N_EXPERTS = N_KEYS * N_KEYS
PEER_TOPK = 16

LANES = 128
VMEM_LIMIT = 56 * 1024 * 1024

F32 = jnp.float32
BF16 = jnp.bfloat16
_NT = (((1,), (1,)), ((), ()))


def _params(sem):
    return pltpu.CompilerParams(dimension_semantics=sem, vmem_limit_bytes=VMEM_LIMIT)


def _rmsnorm_kernel(x_ref, g_ref, o_ref):
    x = x_ref[...]
    ms = jnp.mean(x * x, axis=-1, keepdims=True)
    o_ref[...] = (x * lax.rsqrt(ms + RMS_EPS) * g_ref[...]).astype(o_ref.dtype)


def _rmsnorm(x, gain, tm=512):
    t, d = x.shape
    return pl.pallas_call(
        _rmsnorm_kernel,
        grid=(t // tm,),
        in_specs=[pl.BlockSpec((tm, d), lambda i: (i, 0)), pl.BlockSpec((1, d), lambda i: (0, 0))],
        out_specs=pl.BlockSpec((tm, d), lambda i: (i, 0)),
        out_shape=jax.ShapeDtypeStruct((t, d), BF16),
        compiler_params=_params(("parallel",)),
        name="rmsnorm",
    )(x, gain.reshape(1, d))


def _proj_qk_kernel(x_ref, w_ref, g_ref, cos_ref, sin_ref, o_ref, *, rope):
    acc = jnp.dot(x_ref[...], w_ref[...], preferred_element_type=F32)
    for s in range(acc.shape[1] // HEAD_DIM):
        cs = slice(s * HEAD_DIM, (s + 1) * HEAD_DIM)
        seg = acc[:, cs]
        ms = jnp.mean(seg * seg, axis=-1, keepdims=True)
        y = seg * lax.rsqrt(ms + RMS_EPS) * g_ref[:, cs]
        if rope:
            y = y * cos_ref[...] + pltpu.roll(y, HEAD_DIM // 2, 1) * sin_ref[...]
        o_ref[:, cs] = y.astype(o_ref.dtype)


def _proj_act_kernel(x_ref, w_ref, b_ref, o_ref, *, act):
    acc = jnp.dot(x_ref[...], w_ref[...], preferred_element_type=F32)
    if act == "sigmoid":
        acc = jax.nn.sigmoid(acc + b_ref[...])
    elif act == "log_sigmoid":
        acc = jax.nn.log_sigmoid(acc + b_ref[...])
    o_ref[...] = acc.astype(o_ref.dtype)


def _proj(xn, w, aux, *, mode, out_dtype, seq, cos=None, sin=None, tm=1024, tn=512):
    t, d = xn.shape
    n = w.shape[1]
    tn = min(tn, n)
    x_spec = pl.BlockSpec((tm, d), lambda i, j: (i, 0))
    w_spec = pl.BlockSpec((d, tn), lambda i, j: (0, j))
    a_spec = pl.BlockSpec((1, tn), lambda i, j: (0, j))
    o_spec = pl.BlockSpec((tm, tn), lambda i, j: (i, j))
    if mode in ("qk_rope", "qk"):
        per_seq = seq // tm
        t_spec = pl.BlockSpec((tm, HEAD_DIM), lambda i, j: (i % per_seq, 0))
        body = functools.partial(_proj_qk_kernel, rope=(mode == "qk_rope"))
        in_specs = [x_spec, w_spec, a_spec, t_spec, t_spec]
        args = (xn, w, aux, cos, sin)
    else:
        body = functools.partial(_proj_act_kernel, act=mode)
        in_specs = [x_spec, w_spec, a_spec]
        args = (xn, w, aux)
    return pl.pallas_call(
        body,
        grid=(t // tm, n // tn),
        in_specs=in_specs,
        out_specs=o_spec,
        out_shape=jax.ShapeDtypeStruct((t, n), out_dtype),
        compiler_params=_params(("parallel", "arbitrary")),
        name="proj_" + mode,
    )(*args)


def _decay_kernel(lf_ref, d_ref, dt_ref):
    s = lf_ref.shape[0]
    r = lax.broadcasted_iota(jnp.int32, (BLOCK, BLOCK), 0)
    c = lax.broadcasted_iota(jnp.int32, (BLOCK, BLOCK), 1)
    tri = jnp.where(c <= r, 1.0, 0.0).astype(BF16)
    carry = jnp.zeros((1, LANES), F32)
    for j in range(s // BLOCK):
        rows = slice(j * BLOCK, (j + 1) * BLOCK)
        blk = lf_ref[rows, :]
        hi = blk.astype(BF16)
        r1 = blk - hi.astype(F32)
        mid = r1.astype(BF16)
        lo = (r1 - mid.astype(F32)).astype(BF16)
        cs = (jnp.dot(tri, hi, preferred_element_type=F32)
              + jnp.dot(tri, mid, preferred_element_type=F32)
              + jnp.dot(tri, lo, preferred_element_type=F32)) + carry
        d_ref[rows, :] = cs
        carry = cs[BLOCK - 1:BLOCK, :]
    dt_ref[...] = d_ref[...].T[:FOX_HEADS, :]


def _decay(log_f, batch, seq):
    return pl.pallas_call(
        _decay_kernel,
        grid=(batch,),
        in_specs=[pl.BlockSpec((seq, LANES), lambda b: (b, 0))],
        out_specs=[pl.BlockSpec((seq, LANES), lambda b: (b, 0)),
                   pl.BlockSpec((None, FOX_HEADS, seq), lambda b: (b, 0, 0))],
        out_shape=[jax.ShapeDtypeStruct((batch * seq, LANES), F32),
                   jax.ShapeDtypeStruct((batch, FOX_HEADS, seq), F32)],
        compiler_params=_params(("parallel",)),
        name="fox_decay",
    )(log_f)


def _fox_kernel(q_ref, k_ref, v_ref, dq_ref, dk_ref, o_ref):
    tq = q_ref.shape[0]
    s_len = k_ref.shape[0]
    scale = HEAD_DIM ** -0.5
    q_pos = pl.program_id(1) * tq + lax.broadcasted_iota(jnp.int32, (tq, s_len), 0)
    k_pos = lax.broadcasted_iota(jnp.int32, (tq, s_len), 1)
    causal = k_pos <= q_pos
    for h in range(FOX_HEADS):
        cs = slice(h * HEAD_DIM, (h + 1) * HEAD_DIM)
        s = lax.dot_general(q_ref[:, cs], k_ref[:, cs], _NT, preferred_element_type=F32) * scale
        s = s + dq_ref[:, h:h + 1] - dk_ref[h:h + 1, :]
        s = jnp.where(causal, s, NEG_INF)
        m = jnp.max(s, axis=-1, keepdims=True)
        p = jnp.exp(s - m)
        l = jnp.sum(p, axis=-1, keepdims=True)
        o = jnp.dot(p.astype(BF16), v_ref[:, cs], preferred_element_type=F32) / l
        o_ref[:, cs] = o.astype(o_ref.dtype)


def _fox(qk, v, d_nat, d_t, batch, seq, tq=256):
    t = qk.shape[0]
    nq = seq // tq
    return pl.pallas_call(
        _fox_kernel,
        grid=(batch, nq),
        in_specs=[pl.BlockSpec((tq, B_OUT), lambda b, i: (b * nq + i, 0)),
                  pl.BlockSpec((seq, B_OUT), lambda b, i: (b, 1)),
                  pl.BlockSpec((seq, B_OUT), lambda b, i: (b, 0)),
                  pl.BlockSpec((tq, LANES), lambda b, i: (b * nq + i, 0)),
                  pl.BlockSpec((None, FOX_HEADS, seq), lambda b, i: (b, 0, 0))],
        out_specs=pl.BlockSpec((tq, B_OUT), lambda b, i: (b * nq + i, 0)),
        out_shape=jax.ShapeDtypeStruct((t, B_OUT), BF16),
        compiler_params=_params(("parallel", "arbitrary")),
        name="fox_attention",
    )(qk, qk, v, d_nat, d_t)


def _dil_kernel(q_ref, k_ref, v_ref, o_ref, l_ref):
    nb = q_ref.shape[0] // BLOCK
    scale = HEAD_DIM ** -0.5
    qi = lax.broadcasted_iota(jnp.int32, (BLOCK, 2 * BLOCK), 0)
    kj = lax.broadcasted_iota(jnp.int32, (BLOCK, 2 * BLOCK), 1)
    band = (kj >= qi) & (kj <= qi + BLOCK)
    first = (lax.broadcasted_iota(jnp.int32, (BLOCK, BLOCK), 1)
             <= lax.broadcasted_iota(jnp.int32, (BLOCK, BLOCK), 0))
    for hh in range(DIL_HEADS_PER_GROUP):
        cs = slice(hh * HEAD_DIM, (hh + 1) * HEAD_DIM)
        for n in range(nb):
            rows = slice(n * BLOCK, (n + 1) * BLOCK)
            krows = rows if n == 0 else slice((n - 1) * BLOCK, (n + 1) * BLOCK)
            mask = first if n == 0 else band
            s = lax.dot_general(q_ref[rows, cs], k_ref[krows, cs], _NT, preferred_element_type=F32) * scale
            s = jnp.where(mask, s, NEG_INF)
            m = jnp.max(s, axis=-1, keepdims=True)
            p = jnp.exp(s - m)
            l = jnp.sum(p, axis=-1, keepdims=True)
            o_ref[rows, cs] = jnp.dot(p.astype(BF16), v_ref[krows, cs], preferred_element_type=F32) / l
            l_ref[rows, cs] = jnp.broadcast_to(m + jnp.log(l), (BLOCK, HEAD_DIM))


def _dilated(qk, v, group, dilation, batch, seq):
    s_sub = seq // dilation
    rows = batch * s_sub
    qk_v = qk.reshape(rows, dilation * 2 * DIL_HEADS * HEAD_DIM)
    v_v = v.reshape(rows, dilation * DIL_HEADS * HEAD_DIM)
    n_g = len(DIL_GROUPS)
    spec = lambda f: pl.BlockSpec((s_sub, A_OUT), f)
    o, lse = pl.pallas_call(
        _dil_kernel,
        grid=(batch, dilation),
        in_specs=[spec(lambda b, c: (b, c * 2 * n_g + group)),
                  spec(lambda b, c: (b, c * 2 * n_g + n_g + group)),
                  spec(lambda b, c: (b, c * n_g + group))],
        out_specs=[spec(lambda b, c: (b, c)), spec(lambda b, c: (b, c))],
        out_shape=[jax.ShapeDtypeStruct((rows, dilation * A_OUT), F32)] * 2,
        compiler_params=_params(("parallel", "arbitrary")),
        name="dilated_attention_g%d" % group,
    )(qk_v, qk_v, v_v)
    return o.reshape(batch * seq, A_OUT), lse.reshape(batch * seq, A_OUT)


def _mix_kernel(o1_ref, o2_ref, o3_ref, l1_ref, l2_ref, l3_ref, ob_ref, gate_ref, x_ref,
                wa_ref, wb_ref, wo_ref, g_ref, xo_ref, xn_ref):
    l1, l2, l3 = l1_ref[...], l2_ref[...], l3_ref[...]
    mx = jnp.maximum(jnp.maximum(l1, l2), l3)
    e1, e2, e3 = jnp.exp(l1 - mx), jnp.exp(l2 - mx), jnp.exp(l3 - mx)
    den = e1 + e2 + e3
    o_a = (e1 / den) * o1_ref[...] + (e2 / den) * o2_ref[...] + (e3 / den) * o3_ref[...]
    br_a = jnp.dot(o_a.astype(BF16), wa_ref[...], preferred_element_type=F32)
    br_b = jnp.dot(ob_ref[...], wb_ref[...], preferred_element_type=F32)
    merged = gate_ref[:, :D_MODEL] * br_a + gate_ref[:, D_MODEL:] * br_b
    x_new = x_ref[...] + jnp.dot(merged.astype(BF16), wo_ref[...], preferred_element_type=F32)
    xo_ref[...] = x_new
    ms = jnp.mean(x_new * x_new, axis=-1, keepdims=True)
    xn_ref[...] = (x_new * lax.rsqrt(ms + RMS_EPS) * g_ref[...]).astype(xn_ref.dtype)


def _mix(outs, lses, o_b, gate, x, w_a, w_b, w_o, ffn_gain, tm=256):
    t = x.shape[0]
    row = lambda w: pl.BlockSpec((tm, w), lambda i: (i, 0))
    full = lambda a: pl.BlockSpec(a.shape, lambda i: (0, 0))
    gain = ffn_gain.reshape(1, D_MODEL)
    return pl.pallas_call(
        _mix_kernel,
        grid=(t // tm,),
        in_specs=[row(A_OUT)] * 6 + [row(B_OUT), row(2 * D_MODEL), row(D_MODEL),
                                     full(w_a), full(w_b), full(w_o), full(gain)],
        out_specs=[row(D_MODEL), row(D_MODEL)],
        out_shape=[jax.ShapeDtypeStruct((t, D_MODEL), F32), jax.ShapeDtypeStruct((t, D_MODEL), BF16)],
        compiler_params=_params(("parallel",)),
        name="branch_mix",
    )(*outs, *lses, o_b, gate, x, w_a, w_b, w_o, gain)


def _top16(s, iota_n, row16):
    work = s
    rank = jnp.full(s.shape, float(PEER_TOPK), F32)
    tops = jnp.zeros((PEER_TOPK, LANES), F32)
    for k in range(PEER_TOPK):
        mx = jnp.max(work, axis=0, keepdims=True)
        idx = jnp.min(jnp.where(work == mx, iota_n, float(N_KEYS)), axis=0, keepdims=True)
        sel = iota_n == idx
        rank = jnp.where(sel, float(k), rank)
        work = jnp.where(sel, -jnp.inf, work)
        tops = jnp.where(row16 == float(k), mx, tops)
    return rank, tops


def _staircase(ts0, ts1, row16):
    cnt = jnp.zeros((PEER_TOPK, LANES), F32)
    f1 = jnp.broadcast_to(ts1[0:1, :], (PEER_TOPK, LANES))
    smax = None
    z = None
    for k in range(PEER_TOPK):
        front = jnp.where(cnt < float(PEER_TOPK), ts0 + f1, -jnp.inf)
        mx = jnp.max(front, axis=0, keepdims=True)
        if k == 0:
            smax = mx
            z = jnp.ones_like(mx)
        else:
            z = z + jnp.exp(mx - smax)
        idx = jnp.min(jnp.where(front == mx, row16, float(PEER_TOPK)), axis=0, keepdims=True)
        cnt = jnp.where(row16 == idx, cnt + 1.0, cnt)
        f1 = jnp.zeros_like(f1)
        for b in range(PEER_TOPK):
            f1 = jnp.where(cnt == float(b), ts1[b:b + 1, :], f1)
    return cnt, z


def _peer_select_kernel(xn_ref, wq_ref, keys_ref, a_ref, cnt_ref, b_ref, r2_ref, q_scr, s_scr):
    h = pl.program_id(1)
    tm = xn_ref.shape[0]

    @pl.when(h == 0)
    def _():
        query = jnp.dot(xn_ref[...], wq_ref[...], preferred_element_type=F32)
        for hc in range(2 * PEER_HEADS):
            q_scr[hc] = query[:, hc * N_KEYS:(hc + 1) * N_KEYS].astype(BF16)

    for c in range(2):
        s_scr[c] = lax.dot_general(keys_ref[2 * h + c], q_scr[2 * h + c], _NT, preferred_element_type=F32)

    iota_n = lax.broadcasted_iota(jnp.int32, (N_KEYS, LANES), 0).astype(F32)
    row16 = lax.broadcasted_iota(jnp.int32, (PEER_TOPK, LANES), 0).astype(F32)
    for lg in range(tm // LANES):
        ls = slice(lg * LANES, (lg + 1) * LANES)
        s1 = s_scr[0, :, ls]
        s2 = s_scr[1, :, ls]
        rank1, ts0 = _top16(s1, iota_n, row16)
        rank2, ts1 = _top16(s2, iota_n, row16)
        cnt, z = _staircase(ts0, ts1, row16)
        cnt_n = jnp.zeros_like(s1)
        for a in range(PEER_TOPK):
            cnt_n = jnp.where(rank1 == float(a), cnt[a:a + 1, :], cnt_n)
        a_ref[:, ls] = jnp.exp(s1 - ts0[0:1, :]) / z
        b_ref[:, ls] = jnp.exp(s2 - ts1[0:1, :])
        cnt_ref[:, ls] = cnt_n
        r2_ref[:, ls] = rank2


def _peer_select(xn, w_query, keys, tm=512):
    t = xn.shape[0]
    out_spec = pl.BlockSpec((None, N_KEYS, tm), lambda i, h: (h, 0, i))
    out = jax.ShapeDtypeStruct((PEER_HEADS, N_KEYS, t), F32)
    return pl.pallas_call(
        _peer_select_kernel,
        grid=(t // tm, PEER_HEADS),
        in_specs=[pl.BlockSpec((tm, D_MODEL), lambda i, h: (i, 0)),
                  pl.BlockSpec(w_query.shape, lambda i, h: (0, 0)),
                  pl.BlockSpec(keys.shape, lambda i, h: (0, 0, 0))],
        out_specs=[out_spec] * 4,
        out_shape=[out] * 4,
        scratch_shapes=[pltpu.VMEM((2 * PEER_HEADS, tm, N_KEYS), BF16),
                        pltpu.VMEM((2, N_KEYS, tm), F32)],
        compiler_params=_params(("parallel", "arbitrary")),
        name="peer_select",
    )(xn, w_query, keys)


def _peer_expert_kernel(xn_ref, u_ref, vt_ref, a_ref, cnt_ref, b_ref, r2_ref, x_ref, o_ref,
                        acc_ref, h_ref, p_ref):
    j = pl.program_id(1)
    tm = xn_ref.shape[0]
    nk = u_ref.shape[0] // N_KEYS

    @pl.when(j == 0)
    def _():
        acc_ref[...] = jnp.zeros_like(acc_ref)

    h_ref[...] = lax.dot_general(u_ref[...], xn_ref[...], _NT, preferred_element_type=F32)
    for jj in range(nk):
        rows = slice(jj * N_KEYS, (jj + 1) * N_KEYS)
        for lg in range(tm // LANES):
            ls = slice(lg * LANES, (lg + 1) * LANES)
            g = jnp.zeros((N_KEYS, LANES), F32)
            for h in range(PEER_HEADS):
                taken = r2_ref[h, :, ls] < cnt_ref[h, jj:jj + 1, ls]
                g = g + jnp.where(taken, a_ref[h, jj:jj + 1, ls] * b_ref[h, :, ls], 0.0)
            hv = h_ref[rows, ls]
            act = 0.5 * hv * (1.0 + lax.erf(hv * (2.0 ** -0.5)))
            p_ref[rows, ls] = (g * act).astype(BF16)
    acc_ref[...] += jnp.dot(vt_ref[...], p_ref[...], preferred_element_type=F32)

    @pl.when(j == pl.num_programs(1) - 1)
    def _():
        o_ref[...] = x_ref[...] + acc_ref[...].T


def _peer_experts(xn, u, vt, a, cnt, b, r2, x, tm=512, te=1024):
    t = xn.shape[0]
    nk = te // N_KEYS
    row_spec = pl.BlockSpec((PEER_HEADS, nk, tm), lambda i, j: (0, j, i))
    all_spec = pl.BlockSpec((PEER_HEADS, N_KEYS, tm), lambda i, j: (0, 0, i))
    return pl.pallas_call(
        _peer_expert_kernel,
        grid=(t // tm, N_EXPERTS // te),
        in_specs=[pl.BlockSpec((tm, D_MODEL), lambda i, j: (i, 0)),
                  pl.BlockSpec((te, D_MODEL), lambda i, j: (j, 0)),
                  pl.BlockSpec((D_MODEL, te), lambda i, j: (0, j)),
                  row_spec, row_spec, all_spec, all_spec,
                  pl.BlockSpec((tm, D_MODEL), lambda i, j: (i, 0))],
        out_specs=pl.BlockSpec((tm, D_MODEL), lambda i, j: (i, 0)),
        out_shape=jax.ShapeDtypeStruct((t, D_MODEL), F32),
        scratch_shapes=[pltpu.VMEM((D_MODEL, tm), F32),
                        pltpu.VMEM((te, tm), F32),
                        pltpu.VMEM((te, tm), BF16)],
        compiler_params=_params(("parallel", "arbitrary")),
        name="peer_experts",
    )(xn, u, vt, a, cnt, b, r2, x)


def _rope_tables(seq):
    half = HEAD_DIM // 2
    inv_freq = ROPE_THETA ** (-jnp.arange(half, dtype=F32) / half)
    ang = jnp.arange(seq).astype(F32)[:, None] * inv_freq[None, :]
    cos, sin = jnp.cos(ang), jnp.sin(ang)
    return jnp.concatenate([cos, cos], axis=-1), jnp.concatenate([-sin, sin], axis=-1)


def kernel(x, mix_norm, w_in, b_forget, b_gate, q_norm, k_norm, w_branch_a, w_branch_b, w_out,
           ffn_norm, w_query, sub_keys, expert_u, expert_v):
    batch, seq, dm = x.shape
    t = batch * seq
    depth = w_in.shape[0]
    cos, sin = _rope_tables(seq)
    hd = HEAD_DIM
    nd, nh = DIL_HEADS, N_HEADS
    xf = x.reshape(t, dm)
    xn = _rmsnorm(xf, mix_norm[0])
    for l in range(depth):
        wl = w_in[l]
        wq, wk, wv = wl[:, :nh * hd], wl[:, nh * hd:2 * nh * hd], wl[:, 2 * nh * hd:QKV_COLS]
        w_rope = jnp.concatenate([wq[:, :nd * hd], wk[:, :nd * hd]], axis=1).astype(BF16)
        w_fox = jnp.concatenate([wq[:, nd * hd:], wk[:, nd * hd:]], axis=1).astype(BF16)
        g_rope = jnp.concatenate([q_norm[l, :nd].reshape(1, -1), k_norm[l, :nd].reshape(1, -1)], axis=1)
        g_fox = jnp.concatenate([q_norm[l, nd:].reshape(1, -1), k_norm[l, nd:].reshape(1, -1)], axis=1)
        w_f = jnp.pad(wl[:, QKV_COLS:QKV_COLS + FOX_HEADS], ((0, 0), (0, LANES - FOX_HEADS))).astype(BF16)
        b_f = jnp.pad(b_forget[l], (0, LANES - FOX_HEADS)).reshape(1, LANES)
        w_g = wl[:, QKV_COLS + FOX_HEADS:].astype(BF16)

        qk_a = _proj(xn, w_rope, g_rope, mode="qk_rope", out_dtype=BF16, seq=seq, cos=cos, sin=sin)
        qk_b = _proj(xn, w_fox, g_fox, mode="qk", out_dtype=BF16, seq=seq, cos=cos, sin=sin)
        zeros = lambda n: jnp.zeros((1, n), F32)
        v_a = _proj(xn, wv[:, :nd * hd].astype(BF16), zeros(nd * hd), mode="none", out_dtype=BF16, seq=seq)
        v_b = _proj(xn, wv[:, nd * hd:].astype(BF16), zeros(B_OUT), mode="none", out_dtype=BF16, seq=seq)
        gate = _proj(xn, w_g, b_gate[l].reshape(1, -1), mode="sigmoid", out_dtype=F32, seq=seq)
        log_f = _proj(xn, w_f, b_f, mode="log_sigmoid", out_dtype=F32, seq=seq)

        outs, lses = [], []
        for gi, (_, dilation) in enumerate(DIL_GROUPS):
            o_g, l_g = _dilated(qk_a, v_a, gi, dilation, batch, seq)
            outs.append(o_g)
            lses.append(l_g)
        d_nat, d_t = _decay(log_f, batch, seq)
        o_b = _fox(qk_b, v_b, d_nat, d_t, batch, seq)

        xf, xn2 = _mix(outs, lses, o_b, gate, xf, w_branch_a[l].astype(BF16), w_branch_b[l].astype(BF16),
                       w_out[l].astype(BF16), ffn_norm[l])

        keys = sub_keys[l].reshape(2 * PEER_HEADS, N_KEYS, N_KEYS).astype(BF16)
        a, cnt, b, r2 = _peer_select(xn2, w_query[l].astype(BF16), keys)
        xf = _peer_experts(xn2, expert_u[l].astype(BF16), expert_v[l].T.astype(BF16), a, cnt, b, r2, xf)
        if l + 1 < depth:
            xn = _rmsnorm(xf, mix_norm[l + 1])
    return xf.reshape(batch, seq, dm)
```

```python
import functools

import jax
import jax.numpy as jnp
from jax import lax
from jax.experimental import pallas as pl
from jax.experimental.pallas import tpu as pltpu

D_MODEL = 2048
HEAD_DIM = 128
DIL_GROUPS = ((128, 1), (512, 4), (2048, 16))
DIL_HEADS_PER_GROUP = 4
DIL_HEADS = DIL_HEADS_PER_GROUP * len(DIL_GROUPS)
FOX_HEADS = 8
N_HEADS = DIL_HEADS + FOX_HEADS
QKV_COLS = 3 * N_HEADS * HEAD_DIM
A_OUT = DIL_HEADS_PER_GROUP * HEAD_DIM
B_OUT = FOX_HEADS * HEAD_DIM
BLOCK = 128
ROPE_THETA = 10000.0
NEG_INF = -1e30
RMS_EPS = 1e-6
PEER_HEADS = 8
N_KEYS = 128
N_EXPERTS = N_KEYS * N_KEYS
PEER_TOPK = 16

LANES = 128
VMEM_LIMIT = 56 * 1024 * 1024

F32 = jnp.float32
BF16 = jnp.bfloat16
_NT = (((1,), (1,)), ((), ()))


def _params(sem):
    return pltpu.CompilerParams(dimension_semantics=sem, vmem_limit_bytes=VMEM_LIMIT)


def _rmsnorm_kernel(x_ref, g_ref, o_ref):
    x = x_ref[...]
    ms = jnp.mean(x * x, axis=-1, keepdims=True)
    o_ref[...] = (x * lax.rsqrt(ms + RMS_EPS) * g_ref[...]).astype(o_ref.dtype)


def _rmsnorm(x, gain, tm=512):
    t, d = x.shape
    return pl.pallas_call(
        _rmsnorm_kernel,
        grid=(t // tm,),
        in_specs=[pl.BlockSpec((tm, d), lambda i: (i, 0)), pl.BlockSpec((1, d), lambda i: (0, 0))],
        out_specs=pl.BlockSpec((tm, d), lambda i: (i, 0)),
        out_shape=jax.ShapeDtypeStruct((t, d), BF16),
        compiler_params=_params(("parallel",)),
        name="rmsnorm",
    )(x, gain.reshape(1, d))


def _proj_qk_kernel(x_ref, w_ref, g_ref, cos_ref, sin_ref, o_ref, *, rope):
    acc = jnp.dot(x_ref[...], w_ref[...], preferred_element_type=F32)
    for s in range(acc.shape[1] // HEAD_DIM):
        cs = slice(s * HEAD_DIM, (s + 1) * HEAD_DIM)
        seg = acc[:, cs]
        ms = jnp.mean(seg * seg, axis=-1, keepdims=True)
        y = seg * lax.rsqrt(ms + RMS_EPS) * g_ref[:, cs]
        if rope:
            y = y * cos_ref[...] + pltpu.roll(y, HEAD_DIM // 2, 1) * sin_ref[...]
        o_ref[:, cs] = y.astype(o_ref.dtype)


def _proj_act_kernel(x_ref, w_ref, b_ref, o_ref, *, act):
    acc = jnp.dot(x_ref[...], w_ref[...], preferred_element_type=F32)
    if act == "sigmoid":
        acc = jax.nn.sigmoid(acc + b_ref[...])
    elif act == "log_sigmoid":
        acc = jax.nn.log_sigmoid(acc + b_ref[...])
    o_ref[...] = acc.astype(o_ref.dtype)


def _proj(xn, w, aux, *, mode, out_dtype, seq, cos=None, sin=None, tm=1024, tn=512):
    t, d = xn.shape
    n = w.shape[1]
    tn = min(tn, n)
    x_spec = pl.BlockSpec((tm, d), lambda i, j: (i, 0))
    w_spec = pl.BlockSpec((d, tn), lambda i, j: (0, j))
    a_spec = pl.BlockSpec((1, tn), lambda i, j: (0, j))
    o_spec = pl.BlockSpec((tm, tn), lambda i, j: (i, j))
    if mode in ("qk_rope", "qk"):
        per_seq = seq // tm
        t_spec = pl.BlockSpec((tm, HEAD_DIM), lambda i, j: (i % per_seq, 0))
        body = functools.partial(_proj_qk_kernel, rope=(mode == "qk_rope"))
        in_specs = [x_spec, w_spec, a_spec, t_spec, t_spec]
        args = (xn, w, aux, cos, sin)
    else:
        body = functools.partial(_proj_act_kernel, act=mode)
        in_specs = [x_spec, w_spec, a_spec]
        args = (xn, w, aux)
    return pl.pallas_call(
        body,
        grid=(t // tm, n // tn),
        in_specs=in_specs,
        out_specs=o_spec,
        out_shape=jax.ShapeDtypeStruct((t, n), out_dtype),
        compiler_params=_params(("parallel", "arbitrary")),
        name="proj_" + mode,
    )(*args)


def _decay_kernel(lf_ref, d_ref, dt_ref):
    s = lf_ref.shape[0]
    r = lax.broadcasted_iota(jnp.int32, (BLOCK, BLOCK), 0)
    c = lax.broadcasted_iota(jnp.int32, (BLOCK, BLOCK), 1)
    tri = jnp.where(c <= r, 1.0, 0.0).astype(BF16)
    carry = jnp.zeros((1, LANES), F32)
    for j in range(s // BLOCK):
        rows = slice(j * BLOCK, (j + 1) * BLOCK)
        blk = lf_ref[rows, :]
        hi = blk.astype(BF16)
        r1 = blk - hi.astype(F32)
        mid = r1.astype(BF16)
        lo = (r1 - mid.astype(F32)).astype(BF16)
        cs = (jnp.dot(tri, hi, preferred_element_type=F32)
              + jnp.dot(tri, mid, preferred_element_type=F32)
              + jnp.dot(tri, lo, preferred_element_type=F32)) + carry
        d_ref[rows, :] = cs
        carry = cs[BLOCK - 1:BLOCK, :]
    dt_ref[...] = d_ref[...].T[:FOX_HEADS, :]


def _decay(log_f, batch, seq):
    return pl.pallas_call(
        _decay_kernel,
        grid=(batch,),
        in_specs=[pl.BlockSpec((seq, LANES), lambda b: (b, 0))],
        out_specs=[pl.BlockSpec((seq, LANES), lambda b: (b, 0)),
                   pl.BlockSpec((None, FOX_HEADS, seq), lambda b: (b, 0, 0))],
        out_shape=[jax.ShapeDtypeStruct((batch * seq, LANES), F32),
                   jax.ShapeDtypeStruct((batch, FOX_HEADS, seq), F32)],
        compiler_params=_params(("parallel",)),
        name="fox_decay",
    )(log_f)


def _fox_kernel(q_ref, k_ref, v_ref, dq_ref, dk_ref, o_ref):
    tq = q_ref.shape[0]
    s_len = k_ref.shape[0]
    scale = HEAD_DIM ** -0.5
    q_pos = pl.program_id(1) * tq + lax.broadcasted_iota(jnp.int32, (tq, s_len), 0)
    k_pos = lax.broadcasted_iota(jnp.int32, (tq, s_len), 1)
    causal = k_pos <= q_pos
    for h in range(FOX_HEADS):
        cs = slice(h * HEAD_DIM, (h + 1) * HEAD_DIM)
        s = lax.dot_general(q_ref[:, cs], k_ref[:, cs], _NT, preferred_element_type=F32) * scale
        s = s + dq_ref[:, h:h + 1] - dk_ref[h:h + 1, :]
        s = jnp.where(causal, s, NEG_INF)
        m = jnp.max(s, axis=-1, keepdims=True)
        p = jnp.exp(s - m)
        l = jnp.sum(p, axis=-1, keepdims=True)
        o = jnp.dot(p.astype(BF16), v_ref[:, cs], preferred_element_type=F32) / l
        o_ref[:, cs] = o.astype(o_ref.dtype)


def _fox(qk, v, d_nat, d_t, batch, seq, tq=256):
    t = qk.shape[0]
    nq = seq // tq
    return pl.pallas_call(
        _fox_kernel,
        grid=(batch, nq),
        in_specs=[pl.BlockSpec((tq, B_OUT), lambda b, i: (b * nq + i, 0)),
                  pl.BlockSpec((seq, B_OUT), lambda b, i: (b, 1)),
                  pl.BlockSpec((seq, B_OUT), lambda b, i: (b, 0)),
                  pl.BlockSpec((tq, LANES), lambda b, i: (b * nq + i, 0)),
                  pl.BlockSpec((None, FOX_HEADS, seq), lambda b, i: (b, 0, 0))],
        out_specs=pl.BlockSpec((tq, B_OUT), lambda b, i: (b * nq + i, 0)),
        out_shape=jax.ShapeDtypeStruct((t, B_OUT), BF16),
        compiler_params=_params(("parallel", "arbitrary")),
        name="fox_attention",
    )(qk, qk, v, d_nat, d_t)


def _dil_kernel(q_ref, k_ref, v_ref, o_ref, l_ref):
    nb = q_ref.shape[0] // BLOCK
    scale = HEAD_DIM ** -0.5
    qi = lax.broadcasted_iota(jnp.int32, (BLOCK, 2 * BLOCK), 0)
    kj = lax.broadcasted_iota(jnp.int32, (BLOCK, 2 * BLOCK), 1)
    band = (kj >= qi) & (kj <= qi + BLOCK)
    first = (lax.broadcasted_iota(jnp.int32, (BLOCK, BLOCK), 1)
             <= lax.broadcasted_iota(jnp.int32, (BLOCK, BLOCK), 0))
    for hh in range(DIL_HEADS_PER_GROUP):
        cs = slice(hh * HEAD_DIM, (hh + 1) * HEAD_DIM)
        for n in range(nb):
            rows = slice(n * BLOCK, (n + 1) * BLOCK)
            krows = rows if n == 0 else slice((n - 1) * BLOCK, (n + 1) * BLOCK)
            mask = first if n == 0 else band
            s = lax.dot_general(q_ref[rows, cs], k_ref[krows, cs], _NT, preferred_element_type=F32) * scale
            s = jnp.where(mask, s, NEG_INF)
            m = jnp.max(s, axis=-1, keepdims=True)
            p = jnp.exp(s - m)
            l = jnp.sum(p, axis=-1, keepdims=True)
            o_ref[rows, cs] = jnp.dot(p.astype(BF16), v_ref[krows, cs], preferred_element_type=F32) / l
            l_ref[rows, cs] = jnp.broadcast_to(m + jnp.log(l), (BLOCK, HEAD_DIM))


def _dilated(qk, v, group, dilation, batch, seq):
    s_sub = seq // dilation
    rows = batch * s_sub
    qk_v = qk.reshape(rows, dilation * 2 * DIL_HEADS * HEAD_DIM)
    v_v = v.reshape(rows, dilation * DIL_HEADS * HEAD_DIM)
    n_g = len(DIL_GROUPS)
    spec = lambda f: pl.BlockSpec((s_sub, A_OUT), f)
    o, lse = pl.pallas_call(
        _dil_kernel,
        grid=(batch, dilation),
        in_specs=[spec(lambda b, c: (b, c * 2 * n_g + group)),
                  spec(lambda b, c: (b, c * 2 * n_g + n_g + group)),
                  spec(lambda b, c: (b, c * n_g + group))],
        out_specs=[spec(lambda b, c: (b, c)), spec(lambda b, c: (b, c))],
        out_shape=[jax.ShapeDtypeStruct((rows, dilation * A_OUT), F32)] * 2,
        compiler_params=_params(("parallel", "arbitrary")),
        name="dilated_attention_g%d" % group,
    )(qk_v, qk_v, v_v)
    return o.reshape(batch * seq, A_OUT), lse.reshape(batch * seq, A_OUT)


def _mix_kernel(o1_ref, o2_ref, o3_ref, l1_ref, l2_ref, l3_ref, ob_ref, gate_ref, x_ref,
                wa_ref, wb_ref, wo_ref, g_ref, xo_ref, xn_ref):
    l1, l2, l3 = l1_ref[...], l2_ref[...], l3_ref[...]
    mx = jnp.maximum(jnp.maximum(l1, l2), l3)
    e1, e2, e3 = jnp.exp(l1 - mx), jnp.exp(l2 - mx), jnp.exp(l3 - mx)
    den = e1 + e2 + e3
    o_a = (e1 / den) * o1_ref[...] + (e2 / den) * o2_ref[...] + (e3 / den) * o3_ref[...]
    br_a = jnp.dot(o_a.astype(BF16), wa_ref[...], preferred_element_type=F32)
    br_b = jnp.dot(ob_ref[...], wb_ref[...], preferred_element_type=F32)
    merged = gate_ref[:, :D_MODEL] * br_a + gate_ref[:, D_MODEL:] * br_b
    x_new = x_ref[...] + jnp.dot(merged.astype(BF16), wo_ref[...], preferred_element_type=F32)
    xo_ref[...] = x_new
    ms = jnp.mean(x_new * x_new, axis=-1, keepdims=True)
    xn_ref[...] = (x_new * lax.rsqrt(ms + RMS_EPS) * g_ref[...]).astype(xn_ref.dtype)


def _mix(outs, lses, o_b, gate, x, w_a, w_b, w_o, ffn_gain, tm=256):
    t = x.shape[0]
    row = lambda w: pl.BlockSpec((tm, w), lambda i: (i, 0))
    full = lambda a: pl.BlockSpec(a.shape, lambda i: (0, 0))
    gain = ffn_gain.reshape(1, D_MODEL)
    return pl.pallas_call(
        _mix_kernel,
        grid=(t // tm,),
        in_specs=[row(A_OUT)] * 6 + [row(B_OUT), row(2 * D_MODEL), row(D_MODEL),
                                     full(w_a), full(w_b), full(w_o), full(gain)],
        out_specs=[row(D_MODEL), row(D_MODEL)],
        out_shape=[jax.ShapeDtypeStruct((t, D_MODEL), F32), jax.ShapeDtypeStruct((t, D_MODEL), BF16)],
        compiler_params=_params(("parallel",)),
        name="branch_mix",
    )(*outs, *lses, o_b, gate, x, w_a, w_b, w_o, gain)


def _top16(s, iota_n, row16):
    work = s
    rank = jnp.full(s.shape, float(PEER_TOPK), F32)
    tops = jnp.zeros((PEER_TOPK, LANES), F32)
    for k in range(PEER_TOPK):
        mx = jnp.max(work, axis=0, keepdims=True)
        idx = jnp.min(jnp.where(work == mx, iota_n, float(N_KEYS)), axis=0, keepdims=True)
        sel = iota_n == idx
        rank = jnp.where(sel, float(k), rank)
        work = jnp.where(sel, -jnp.inf, work)
        tops = jnp.where(row16 == float(k), mx, tops)
    return rank, tops


def _staircase(ts0, ts1, row16):
    cnt = jnp.zeros((PEER_TOPK, LANES), F32)
    f1 = jnp.broadcast_to(ts1[0:1, :], (PEER_TOPK, LANES))
    smax = None
    z = None
    for k in range(PEER_TOPK):
        front = jnp.where(cnt < float(PEER_TOPK), ts0 + f1, -jnp.inf)
        mx = jnp.max(front, axis=0, keepdims=True)
        if k == 0:
            smax = mx
            z = jnp.ones_like(mx)
        else:
            z = z + jnp.exp(mx - smax)
        idx = jnp.min(jnp.where(front == mx, row16, float(PEER_TOPK)), axis=0, keepdims=True)
        cnt = jnp.where(row16 == idx, cnt + 1.0, cnt)
        f1 = jnp.zeros_like(f1)
        for b in range(PEER_TOPK):
            f1 = jnp.where(cnt == float(b), ts1[b:b + 1, :], f1)
    return cnt, z


def _peer_select_kernel(xn_ref, wq_ref, keys_ref, a_ref, cnt_ref, b_ref, r2_ref, q_scr, s_scr):
    h = pl.program_id(1)
    tm = xn_ref.shape[0]

    @pl.when(h == 0)
    def _():
        query = jnp.dot(xn_ref[...], wq_ref[...], preferred_element_type=F32)
        for hc in range(2 * PEER_HEADS):
            q_scr[hc] = query[:, hc * N_KEYS:(hc + 1) * N_KEYS].astype(BF16)

    for c in range(2):
        s_scr[c] = lax.dot_general(keys_ref[2 * h + c], q_scr[2 * h + c], _NT, preferred_element_type=F32)

    iota_n = lax.broadcasted_iota(jnp.int32, (N_KEYS, LANES), 0).astype(F32)
    row16 = lax.broadcasted_iota(jnp.int32, (PEER_TOPK, LANES), 0).astype(F32)
    for lg in range(tm // LANES):
        ls = slice(lg * LANES, (lg + 1) * LANES)
        s1 = s_scr[0, :, ls]
        s2 = s_scr[1, :, ls]
        rank1, ts0 = _top16(s1, iota_n, row16)
        rank2, ts1 = _top16(s2, iota_n, row16)
        cnt, z = _staircase(ts0, ts1, row16)
        cnt_n = jnp.zeros_like(s1)
        for a in range(PEER_TOPK):
            cnt_n = jnp.where(rank1 == float(a), cnt[a:a + 1, :], cnt_n)
        a_ref[:, ls] = jnp.exp(s1 - ts0[0:1, :]) / z
        b_ref[:, ls] = jnp.exp(s2 - ts1[0:1, :]).astype(b_ref.dtype)
        cnt_ref[:, ls] = cnt_n
        r2_ref[:, ls] = rank2.astype(r2_ref.dtype)


def _peer_select(xn, w_query, keys, tm=512):
    t = xn.shape[0]
    out_spec = pl.BlockSpec((None, N_KEYS, tm), lambda i, h: (h, 0, i))
    out = jax.ShapeDtypeStruct((PEER_HEADS, N_KEYS, t), F32)
    flat_spec = pl.BlockSpec((N_KEYS, tm), lambda i, h: (h, i))
    flat = jax.ShapeDtypeStruct((PEER_HEADS * N_KEYS, t), BF16)
    return pl.pallas_call(
        _peer_select_kernel,
        grid=(t // tm, PEER_HEADS),
        in_specs=[pl.BlockSpec((tm, D_MODEL), lambda i, h: (i, 0)),
                  pl.BlockSpec(w_query.shape, lambda i, h: (0, 0)),
                  pl.BlockSpec(keys.shape, lambda i, h: (0, 0, 0))],
        out_specs=[out_spec, out_spec, flat_spec, flat_spec],
        out_shape=[out, out, flat, flat],
        scratch_shapes=[pltpu.VMEM((2 * PEER_HEADS, tm, N_KEYS), BF16),
                        pltpu.VMEM((2, N_KEYS, tm), F32)],
        compiler_params=_params(("parallel", "arbitrary")),
        name="peer_select",
    )(xn, w_query, keys)


PEER_NK = 8
BF16_ROWS = 16


def _peer_expert_kernel(xn_in_ref, u_ref, vt_ref, a_ref, cnt_ref, b_in_ref, r2_in_ref, x_ref, o_ref,
                        acc_ref, h_ref, p_ref, xs_ref, b_ref, r2_ref):
    j = pl.program_id(1)
    tm = xn_in_ref.shape[0]
    sub = BF16_ROWS

    @pl.when(j == 0)
    def _():
        acc_ref[...] = jnp.zeros_like(acc_ref)
        xs_ref[...] = xn_in_ref[...]
        b_ref[...] = b_in_ref[...]
        r2_ref[...] = r2_in_ref[...]

    h_ref[...] = lax.dot_general(u_ref[...], xs_ref[...], _NT, preferred_element_type=F32)
    for jj in range(PEER_NK):
        for lg in range(tm // LANES):
            ls = slice(lg * LANES, (lg + 1) * LANES)
            row = lambda ref, h: jnp.broadcast_to(ref[h, jj:jj + 1, ls], (sub, LANES)).astype(BF16)
            cnt_rows = [row(cnt_ref, h) for h in range(PEER_HEADS)]
            a_rows = [row(a_ref, h) for h in range(PEER_HEADS)]
            for st in range(N_KEYS // sub):
                terms = []
                for h in range(PEER_HEADS):
                    rs = slice(h * N_KEYS + st * sub, h * N_KEYS + (st + 1) * sub)
                    terms.append(jnp.where(r2_ref[rs, ls] < cnt_rows[h], a_rows[h] * b_ref[rs, ls], 0.0))
                while len(terms) > 1:
                    terms = [terms[i] + terms[i + 1] for i in range(0, len(terms), 2)]
                hs = slice(jj * N_KEYS + st * sub, jj * N_KEYS + (st + 1) * sub)
                hv = h_ref[hs, ls]
                act = 0.5 * hv * (1.0 + lax.erf(hv * (2.0 ** -0.5)))
                p_ref[hs, ls] = terms[0] * act.astype(BF16)
    acc_ref[...] += jnp.dot(vt_ref[...], p_ref[...], preferred_element_type=F32)

    @pl.when(j == pl.num_programs(1) - 1)
    def _():
        o_ref[...] = x_ref[...] + acc_ref[...].T


def _peer_experts(xn, u, vt, a, cnt, b, r2, x, tm=512):
    t = xn.shape[0]
    te = PEER_NK * N_KEYS
    row_spec = pl.BlockSpec((PEER_HEADS, PEER_NK, tm), lambda i, j: (0, j, i))
    all_spec = pl.BlockSpec((PEER_HEADS * N_KEYS, tm), lambda i, j: (0, i))
    return pl.pallas_call(
        _peer_expert_kernel,
        grid=(t // tm, N_EXPERTS // te),
        in_specs=[pl.BlockSpec((tm, D_MODEL), lambda i, j: (i, 0)),
                  pl.BlockSpec((te, D_MODEL), lambda i, j: (j, 0)),
                  pl.BlockSpec((D_MODEL, te), lambda i, j: (0, j)),
                  row_spec, row_spec, all_spec, all_spec,
                  pl.BlockSpec((tm, D_MODEL), lambda i, j: (i, 0))],
        out_specs=pl.BlockSpec((tm, D_MODEL), lambda i, j: (i, 0)),
        out_shape=jax.ShapeDtypeStruct((t, D_MODEL), F32),
        scratch_shapes=[pltpu.VMEM((D_MODEL, tm), F32),
                        pltpu.VMEM((te, tm), F32),
                        pltpu.VMEM((te, tm), BF16),
                        pltpu.VMEM((tm, D_MODEL), BF16),
                        pltpu.VMEM((PEER_HEADS * N_KEYS, tm), BF16),
                        pltpu.VMEM((PEER_HEADS * N_KEYS, tm), BF16)],
        compiler_params=_params(("parallel", "arbitrary")),
        name="peer_experts",
    )(xn, u, vt, a, cnt, b, r2, x)


def _rope_tables(seq):
    half = HEAD_DIM // 2
    inv_freq = ROPE_THETA ** (-jnp.arange(half, dtype=F32) / half)
    ang = jnp.arange(seq).astype(F32)[:, None] * inv_freq[None, :]
    cos, sin = jnp.cos(ang), jnp.sin(ang)
    return jnp.concatenate([cos, cos], axis=-1), jnp.concatenate([-sin, sin], axis=-1)


def kernel(x, mix_norm, w_in, b_forget, b_gate, q_norm, k_norm, w_branch_a, w_branch_b, w_out,
           ffn_norm, w_query, sub_keys, expert_u, expert_v):
    batch, seq, dm = x.shape
    t = batch * seq
    depth = w_in.shape[0]
    cos, sin = _rope_tables(seq)
    hd = HEAD_DIM
    nd, nh = DIL_HEADS, N_HEADS
    xf = x.reshape(t, dm)
    xn = _rmsnorm(xf, mix_norm[0])
    for l in range(depth):
        wl = w_in[l]
        wq, wk, wv = wl[:, :nh * hd], wl[:, nh * hd:2 * nh * hd], wl[:, 2 * nh * hd:QKV_COLS]
        w_rope = jnp.concatenate([wq[:, :nd * hd], wk[:, :nd * hd]], axis=1).astype(BF16)
        w_fox = jnp.concatenate([wq[:, nd * hd:], wk[:, nd * hd:]], axis=1).astype(BF16)
        g_rope = jnp.concatenate([q_norm[l, :nd].reshape(1, -1), k_norm[l, :nd].reshape(1, -1)], axis=1)
        g_fox = jnp.concatenate([q_norm[l, nd:].reshape(1, -1), k_norm[l, nd:].reshape(1, -1)], axis=1)
        w_f = jnp.pad(wl[:, QKV_COLS:QKV_COLS + FOX_HEADS], ((0, 0), (0, LANES - FOX_HEADS))).astype(BF16)
        b_f = jnp.pad(b_forget[l], (0, LANES - FOX_HEADS)).reshape(1, LANES)
        w_g = wl[:, QKV_COLS + FOX_HEADS:].astype(BF16)

        qk_a = _proj(xn, w_rope, g_rope, mode="qk_rope", out_dtype=BF16, seq=seq, cos=cos, sin=sin)
        qk_b = _proj(xn, w_fox, g_fox, mode="qk", out_dtype=BF16, seq=seq, cos=cos, sin=sin)
        zeros = lambda n: jnp.zeros((1, n), F32)
        v_a = _proj(xn, wv[:, :nd * hd].astype(BF16), zeros(nd * hd), mode="none", out_dtype=BF16, seq=seq)
        v_b = _proj(xn, wv[:, nd * hd:].astype(BF16), zeros(B_OUT), mode="none", out_dtype=BF16, seq=seq)
        gate = _proj(xn, w_g, b_gate[l].reshape(1, -1), mode="sigmoid", out_dtype=F32, seq=seq)
        log_f = _proj(xn, w_f, b_f, mode="log_sigmoid", out_dtype=F32, seq=seq)

        outs, lses = [], []
        for gi, (_, dilation) in enumerate(DIL_GROUPS):
            o_g, l_g = _dilated(qk_a, v_a, gi, dilation, batch, seq)
            outs.append(o_g)
            lses.append(l_g)
        d_nat, d_t = _decay(log_f, batch, seq)
        o_b = _fox(qk_b, v_b, d_nat, d_t, batch, seq)

        xf, xn2 = _mix(outs, lses, o_b, gate, xf, w_branch_a[l].astype(BF16), w_branch_b[l].astype(BF16),
                       w_out[l].astype(BF16), ffn_norm[l])

        keys = sub_keys[l].reshape(2 * PEER_HEADS, N_KEYS, N_KEYS).astype(BF16)
        a, cnt, b, r2 = _peer_select(xn2, w_query[l].astype(BF16), keys)
        xf = _peer_experts(xn2, expert_u[l].astype(BF16), expert_v[l].T.astype(BF16), a, cnt, b, r2, xf)
        if l + 1 < depth:
            xn = _rmsnorm(xf, mix_norm[l + 1])
    return xf.reshape(batch, seq, dm)
```

```python
import functools

import jax
import jax.numpy as jnp
from jax import lax
from jax.experimental import pallas as pl
from jax.experimental.pallas import tpu as pltpu

D_MODEL = 2048
HEAD_DIM = 128
DIL_GROUPS = ((128, 1), (512, 4), (2048, 16))
DIL_HEADS_PER_GROUP = 4
DIL_HEADS = DIL_HEADS_PER_GROUP * len(DIL_GROUPS)
FOX_HEADS = 8
N_HEADS = DIL_HEADS + FOX_HEADS
QKV_COLS = 3 * N_HEADS * HEAD_DIM
A_OUT = DIL_HEADS_PER_GROUP * HEAD_DIM
B_OUT = FOX_HEADS * HEAD_DIM
BLOCK = 128
ROPE_THETA = 10000.0
NEG_INF = -1e30
RMS_EPS = 1e-6
PEER_HEADS = 8
N_KEYS = 128
N_EXPERTS = N_KEYS * N_KEYS
PEER_TOPK = 16

LANES = 128
VMEM_LIMIT = 56 * 1024 * 1024
PEER_VMEM_LIMIT = 60 * 1024 * 1024

F32 = jnp.float32
BF16 = jnp.bfloat16
_NT = (((1,), (1,)), ((), ()))


def _params(sem):
    return pltpu.CompilerParams(dimension_semantics=sem, vmem_limit_bytes=VMEM_LIMIT)


def _rmsnorm_kernel(x_ref, g_ref, o_ref):
    x = x_ref[...]
    ms = jnp.mean(x * x, axis=-1, keepdims=True)
    o_ref[...] = (x * lax.rsqrt(ms + RMS_EPS) * g_ref[...]).astype(o_ref.dtype)


def _rmsnorm(x, gain, tm=512):
    t, d = x.shape
    return pl.pallas_call(
        _rmsnorm_kernel,
        grid=(t // tm,),
        in_specs=[pl.BlockSpec((tm, d), lambda i: (i, 0)), pl.BlockSpec((1, d), lambda i: (0, 0))],
        out_specs=pl.BlockSpec((tm, d), lambda i: (i, 0)),
        out_shape=jax.ShapeDtypeStruct((t, d), BF16),
        compiler_params=_params(("parallel",)),
        name="rmsnorm",
    )(x, gain.reshape(1, d))


def _residual_kernel(x_ref, pt_ref, g_ref, xo_ref, xn_ref):
    x_new = x_ref[...] + pt_ref[...].T
    xo_ref[...] = x_new
    if xn_ref is not None:
        ms = jnp.mean(x_new * x_new, axis=-1, keepdims=True)
        xn_ref[...] = (x_new * lax.rsqrt(ms + RMS_EPS) * g_ref[...]).astype(xn_ref.dtype)


def _residual_last_kernel(x_ref, pt_ref, xo_ref):
    _residual_kernel(x_ref, pt_ref, None, xo_ref, None)


def _add_transposed(x, y_t, gain=None, tm=512):
    t, d = x.shape
    row = pl.BlockSpec((tm, d), lambda i: (i, 0))
    col = pl.BlockSpec((d, tm), lambda i: (0, i))
    f32 = jax.ShapeDtypeStruct((t, d), F32)
    if gain is None:
        return pl.pallas_call(
            _residual_last_kernel, grid=(t // tm,), in_specs=[row, col], out_specs=row, out_shape=f32,
            compiler_params=_params(("parallel",)), name="residual_add")(x, y_t)
    return pl.pallas_call(
        _residual_kernel, grid=(t // tm,),
        in_specs=[row, col, pl.BlockSpec((1, d), lambda i: (0, 0))],
        out_specs=[row, row], out_shape=[f32, jax.ShapeDtypeStruct((t, d), BF16)],
        compiler_params=_params(("parallel",)), name="residual_rmsnorm")(x, y_t, gain.reshape(1, d))


def _proj_qk_kernel(x_ref, w_ref, g_ref, cos_ref, sin_ref, o_ref, *, rope):
    acc = jnp.dot(x_ref[...], w_ref[...], preferred_element_type=F32)
    for s in range(acc.shape[1] // HEAD_DIM):
        cs = slice(s * HEAD_DIM, (s + 1) * HEAD_DIM)
        seg = acc[:, cs]
        ms = jnp.mean(seg * seg, axis=-1, keepdims=True)
        y = seg * lax.rsqrt(ms + RMS_EPS) * g_ref[:, cs]
        if rope:
            y = y * cos_ref[...] + pltpu.roll(y, HEAD_DIM // 2, 1) * sin_ref[...]
        o_ref[:, cs] = y.astype(o_ref.dtype)


def _proj_act_kernel(x_ref, w_ref, b_ref, o_ref, *, act):
    acc = jnp.dot(x_ref[...], w_ref[...], preferred_element_type=F32)
    if act == "sigmoid":
        acc = jax.nn.sigmoid(acc + b_ref[...])
    elif act == "log_sigmoid":
        acc = jax.nn.log_sigmoid(acc + b_ref[...])
    o_ref[...] = acc.astype(o_ref.dtype)


def _proj(xn, w, aux, *, mode, out_dtype, seq, cos=None, sin=None, tm=1024, tn=512):
    t, d = xn.shape
    n = w.shape[1]
    tn = min(tn, n)
    x_spec = pl.BlockSpec((tm, d), lambda i, j: (i, 0))
    w_spec = pl.BlockSpec((d, tn), lambda i, j: (0, j))
    a_spec = pl.BlockSpec((1, tn), lambda i, j: (0, j))
    o_spec = pl.BlockSpec((tm, tn), lambda i, j: (i, j))
    if mode in ("qk_rope", "qk"):
        per_seq = seq // tm
        t_spec = pl.BlockSpec((tm, HEAD_DIM), lambda i, j: (i % per_seq, 0))
        body = functools.partial(_proj_qk_kernel, rope=(mode == "qk_rope"))
        in_specs = [x_spec, w_spec, a_spec, t_spec, t_spec]
        args = (xn, w, aux, cos, sin)
    else:
        body = functools.partial(_proj_act_kernel, act=mode)
        in_specs = [x_spec, w_spec, a_spec]
        args = (xn, w, aux)
    return pl.pallas_call(
        body,
        grid=(t // tm, n // tn),
        in_specs=in_specs,
        out_specs=o_spec,
        out_shape=jax.ShapeDtypeStruct((t, n), out_dtype),
        compiler_params=_params(("parallel", "arbitrary")),
        name="proj_" + mode,
    )(*args)


def _decay_kernel(lf_ref, d_ref, dt_ref):
    s = lf_ref.shape[0]
    r = lax.broadcasted_iota(jnp.int32, (BLOCK, BLOCK), 0)
    c = lax.broadcasted_iota(jnp.int32, (BLOCK, BLOCK), 1)
    tri = jnp.where(c <= r, 1.0, 0.0).astype(BF16)
    carry = jnp.zeros((1, LANES), F32)
    for j in range(s // BLOCK):
        rows = slice(j * BLOCK, (j + 1) * BLOCK)
        blk = lf_ref[rows, :]
        hi = blk.astype(BF16)
        r1 = blk - hi.astype(F32)
        mid = r1.astype(BF16)
        lo = (r1 - mid.astype(F32)).astype(BF16)
        cs = (jnp.dot(tri, hi, preferred_element_type=F32)
              + jnp.dot(tri, mid, preferred_element_type=F32)
              + jnp.dot(tri, lo, preferred_element_type=F32)) + carry
        d_ref[rows, :] = cs
        carry = cs[BLOCK - 1:BLOCK, :]
    dt_ref[...] = d_ref[...].T[:FOX_HEADS, :]


def _decay(log_f, batch, seq):
    return pl.pallas_call(
        _decay_kernel,
        grid=(batch,),
        in_specs=[pl.BlockSpec((seq, LANES), lambda b: (b, 0))],
        out_specs=[pl.BlockSpec((seq, LANES), lambda b: (b, 0)),
                   pl.BlockSpec((None, FOX_HEADS, seq), lambda b: (b, 0, 0))],
        out_shape=[jax.ShapeDtypeStruct((batch * seq, LANES), F32),
                   jax.ShapeDtypeStruct((batch, FOX_HEADS, seq), F32)],
        compiler_params=_params(("parallel",)),
        name="fox_decay",
    )(log_f)


def _fox_kernel(q_ref, k_ref, v_ref, dq_ref, dk_ref, o_ref):
    tq = q_ref.shape[0]
    s_len = k_ref.shape[0]
    scale = HEAD_DIM ** -0.5
    q_pos = pl.program_id(1) * tq + lax.broadcasted_iota(jnp.int32, (tq, s_len), 0)
    k_pos = lax.broadcasted_iota(jnp.int32, (tq, s_len), 1)
    causal = k_pos <= q_pos
    for h in range(FOX_HEADS):
        cs = slice(h * HEAD_DIM, (h + 1) * HEAD_DIM)
        s = lax.dot_general(q_ref[:, cs], k_ref[:, cs], _NT, preferred_element_type=F32) * scale
        s = s + dq_ref[:, h:h + 1] - dk_ref[h:h + 1, :]
        s = jnp.where(causal, s, NEG_INF)
        m = jnp.max(s, axis=-1, keepdims=True)
        p = jnp.exp(s - m)
        l = jnp.sum(p, axis=-1, keepdims=True)
        o = jnp.dot(p.astype(BF16), v_ref[:, cs], preferred_element_type=F32) / l
        o_ref[:, cs] = o.astype(o_ref.dtype)


def _fox(qk, v, d_nat, d_t, batch, seq, tq=256):
    t = qk.shape[0]
    nq = seq // tq
    return pl.pallas_call(
        _fox_kernel,
        grid=(batch, nq),
        in_specs=[pl.BlockSpec((tq, B_OUT), lambda b, i: (b * nq + i, 0)),
                  pl.BlockSpec((seq, B_OUT), lambda b, i: (b, 1)),
                  pl.BlockSpec((seq, B_OUT), lambda b, i: (b, 0)),
                  pl.BlockSpec((tq, LANES), lambda b, i: (b * nq + i, 0)),
                  pl.BlockSpec((None, FOX_HEADS, seq), lambda b, i: (b, 0, 0))],
        out_specs=pl.BlockSpec((tq, B_OUT), lambda b, i: (b * nq + i, 0)),
        out_shape=jax.ShapeDtypeStruct((t, B_OUT), BF16),
        compiler_params=_params(("parallel", "arbitrary")),
        name="fox_attention",
    )(qk, qk, v, d_nat, d_t)


def _dil_kernel(q_ref, k_ref, v_ref, o_ref, l_ref):
    nb = q_ref.shape[0] // BLOCK
    scale = HEAD_DIM ** -0.5
    qi = lax.broadcasted_iota(jnp.int32, (BLOCK, 2 * BLOCK), 0)
    kj = lax.broadcasted_iota(jnp.int32, (BLOCK, 2 * BLOCK), 1)
    band = (kj >= qi) & (kj <= qi + BLOCK)
    first = (lax.broadcasted_iota(jnp.int32, (BLOCK, BLOCK), 1)
             <= lax.broadcasted_iota(jnp.int32, (BLOCK, BLOCK), 0))
    for hh in range(DIL_HEADS_PER_GROUP):
        cs = slice(hh * HEAD_DIM, (hh + 1) * HEAD_DIM)
        for n in range(nb):
            rows = slice(n * BLOCK, (n + 1) * BLOCK)
            krows = rows if n == 0 else slice((n - 1) * BLOCK, (n + 1) * BLOCK)
            mask = first if n == 0 else band
            s = lax.dot_general(q_ref[rows, cs], k_ref[krows, cs], _NT, preferred_element_type=F32) * scale
            s = jnp.where(mask, s, NEG_INF)
            m = jnp.max(s, axis=-1, keepdims=True)
            p = jnp.exp(s - m)
            l = jnp.sum(p, axis=-1, keepdims=True)
            o_ref[rows, cs] = jnp.dot(p.astype(BF16), v_ref[krows, cs], preferred_element_type=F32) / l
            l_ref[rows, cs] = jnp.broadcast_to(m + jnp.log(l), (BLOCK, HEAD_DIM))


def _dilated(qk, v, group, dilation, batch, seq):
    s_sub = seq // dilation
    rows = batch * s_sub
    qk_v = qk.reshape(rows, dilation * 2 * DIL_HEADS * HEAD_DIM)
    v_v = v.reshape(rows, dilation * DIL_HEADS * HEAD_DIM)
    n_g = len(DIL_GROUPS)
    spec = lambda f: pl.BlockSpec((s_sub, A_OUT), f)
    o, lse = pl.pallas_call(
        _dil_kernel,
        grid=(batch, dilation),
        in_specs=[spec(lambda b, c: (b, c * 2 * n_g + group)),
                  spec(lambda b, c: (b, c * 2 * n_g + n_g + group)),
                  spec(lambda b, c: (b, c * n_g + group))],
        out_specs=[spec(lambda b, c: (b, c)), spec(lambda b, c: (b, c))],
        out_shape=[jax.ShapeDtypeStruct((rows, dilation * A_OUT), F32)] * 2,
        compiler_params=_params(("parallel", "arbitrary")),
        name="dilated_attention_g%d" % group,
    )(qk_v, qk_v, v_v)
    return o.reshape(batch * seq, A_OUT), lse.reshape(batch * seq, A_OUT)


def _mix_kernel(o1_ref, o2_ref, o3_ref, l1_ref, l2_ref, l3_ref, ob_ref, gate_ref, x_ref,
                wa_ref, wb_ref, wo_ref, g_ref, xo_ref, xn_ref):
    l1, l2, l3 = l1_ref[...], l2_ref[...], l3_ref[...]
    mx = jnp.maximum(jnp.maximum(l1, l2), l3)
    e1, e2, e3 = jnp.exp(l1 - mx), jnp.exp(l2 - mx), jnp.exp(l3 - mx)
    den = e1 + e2 + e3
    o_a = (e1 / den) * o1_ref[...] + (e2 / den) * o2_ref[...] + (e3 / den) * o3_ref[...]
    br_a = jnp.dot(o_a.astype(BF16), wa_ref[...], preferred_element_type=F32)
    br_b = jnp.dot(ob_ref[...], wb_ref[...], preferred_element_type=F32)
    merged = gate_ref[:, :D_MODEL] * br_a + gate_ref[:, D_MODEL:] * br_b
    x_new = x_ref[...] + jnp.dot(merged.astype(BF16), wo_ref[...], preferred_element_type=F32)
    xo_ref[...] = x_new
    ms = jnp.mean(x_new * x_new, axis=-1, keepdims=True)
    xn_ref[...] = (x_new * lax.rsqrt(ms + RMS_EPS) * g_ref[...]).astype(xn_ref.dtype)


def _mix(outs, lses, o_b, gate, x, w_a, w_b, w_o, ffn_gain, tm=256):
    t = x.shape[0]
    row = lambda w: pl.BlockSpec((tm, w), lambda i: (i, 0))
    full = lambda a: pl.BlockSpec(a.shape, lambda i: (0, 0))
    gain = ffn_gain.reshape(1, D_MODEL)
    return pl.pallas_call(
        _mix_kernel,
        grid=(t // tm,),
        in_specs=[row(A_OUT)] * 6 + [row(B_OUT), row(2 * D_MODEL), row(D_MODEL),
                                     full(w_a), full(w_b), full(w_o), full(gain)],
        out_specs=[row(D_MODEL), row(D_MODEL)],
        out_shape=[jax.ShapeDtypeStruct((t, D_MODEL), F32), jax.ShapeDtypeStruct((t, D_MODEL), BF16)],
        compiler_params=_params(("parallel",)),
        name="branch_mix",
    )(*outs, *lses, o_b, gate, x, w_a, w_b, w_o, gain)


def _top16(s, iota_n, row16):
    work = s
    rank = jnp.full(s.shape, float(PEER_TOPK), F32)
    tops = jnp.zeros((PEER_TOPK, LANES), F32)
    for k in range(PEER_TOPK):
        mx = jnp.max(work, axis=0, keepdims=True)
        idx = jnp.min(jnp.where(work == mx, iota_n, float(N_KEYS)), axis=0, keepdims=True)
        sel = iota_n == idx
        rank = jnp.where(sel, float(k), rank)
        work = jnp.where(sel, -jnp.inf, work)
        tops = jnp.where(row16 == float(k), mx, tops)
    return rank, tops


def _staircase(ts0, ts1, row16):
    cnt = jnp.zeros((PEER_TOPK, LANES), F32)
    f1 = jnp.broadcast_to(ts1[0:1, :], (PEER_TOPK, LANES))
    smax = None
    z = None
    for k in range(PEER_TOPK):
        front = jnp.where(cnt < float(PEER_TOPK), ts0 + f1, -jnp.inf)
        mx = jnp.max(front, axis=0, keepdims=True)
        if k == 0:
            smax = mx
            z = jnp.ones_like(mx)
        else:
            z = z + jnp.exp(mx - smax)
        idx = jnp.min(jnp.where(front == mx, row16, float(PEER_TOPK)), axis=0, keepdims=True)
        cnt = jnp.where(row16 == idx, cnt + 1.0, cnt)
        f1 = jnp.zeros_like(f1)
        for b in range(PEER_TOPK):
            f1 = jnp.where(cnt == float(b), ts1[b:b + 1, :], f1)
    return cnt, z


def _peer_select_kernel(xn_ref, wq_ref, keys_ref, a_ref, cnt_ref, b_ref, r2_ref, q_scr, s_scr):
    h = pl.program_id(1)
    tm = xn_ref.shape[0]

    @pl.when(h == 0)
    def _():
        query = jnp.dot(xn_ref[...], wq_ref[...], preferred_element_type=F32)
        for hc in range(2 * PEER_HEADS):
            q_scr[hc] = query[:, hc * N_KEYS:(hc + 1) * N_KEYS].astype(BF16)

    for c in range(2):
        s_scr[c] = lax.dot_general(keys_ref[2 * h + c], q_scr[2 * h + c], _NT, preferred_element_type=F32)

    iota_n = lax.broadcasted_iota(jnp.int32, (N_KEYS, LANES), 0).astype(F32)
    row16 = lax.broadcasted_iota(jnp.int32, (PEER_TOPK, LANES), 0).astype(F32)
    for lg in range(tm // LANES):
        ls = slice(lg * LANES, (lg + 1) * LANES)
        s1 = s_scr[0, :, ls]
        s2 = s_scr[1, :, ls]
        rank1, ts0 = _top16(s1, iota_n, row16)
        rank2, ts1 = _top16(s2, iota_n, row16)
        cnt, z = _staircase(ts0, ts1, row16)
        cnt_n = jnp.zeros_like(s1)
        for a in range(PEER_TOPK):
            cnt_n = jnp.where(rank1 == float(a), cnt[a:a + 1, :], cnt_n)
        a_ref[:, ls] = jnp.exp(s1 - ts0[0:1, :]) / z
        b_ref[:, ls] = jnp.exp(s2 - ts1[0:1, :]).astype(b_ref.dtype)
        cnt_ref[:, ls] = cnt_n
        r2_ref[:, ls] = rank2.astype(r2_ref.dtype)


def _peer_select(xn, w_query, keys, tm=512):
    t = xn.shape[0]
    out_spec = pl.BlockSpec((None, N_KEYS, tm), lambda i, h: (h, 0, i))
    out = jax.ShapeDtypeStruct((PEER_HEADS, N_KEYS, t), F32)
    flat_spec = pl.BlockSpec((N_KEYS, tm), lambda i, h: (h, i))
    flat = jax.ShapeDtypeStruct((PEER_HEADS * N_KEYS, t), BF16)
    return pl.pallas_call(
        _peer_select_kernel,
        grid=(t // tm, PEER_HEADS),
        in_specs=[pl.BlockSpec((tm, D_MODEL), lambda i, h: (i, 0)),
                  pl.BlockSpec(w_query.shape, lambda i, h: (0, 0)),
                  pl.BlockSpec(keys.shape, lambda i, h: (0, 0, 0))],
        out_specs=[out_spec, out_spec, flat_spec, flat_spec],
        out_shape=[out, out, flat, flat],
        scratch_shapes=[pltpu.VMEM((2 * PEER_HEADS, tm, N_KEYS), BF16),
                        pltpu.VMEM((2, N_KEYS, tm), F32)],
        compiler_params=_params(("parallel", "arbitrary")),
        name="peer_select",
    )(xn, w_query, keys)


PEER_NK = 8
BF16_ROWS = 16


def _peer_expert_kernel(xn_ref, u_ref, vt_ref, a_ref, cnt_ref, b_in_ref, r2_in_ref, o_ref,
                        h_ref, p_ref, b_ref, r2_ref):
    j = pl.program_id(1)
    tm = xn_ref.shape[0]
    sub = BF16_ROWS

    @pl.when(j == 0)
    def _():
        o_ref[...] = jnp.zeros_like(o_ref)
        b_ref[...] = b_in_ref[...]
        r2_ref[...] = r2_in_ref[...]

    h_ref[...] = lax.dot_general(u_ref[...], xn_ref[...], _NT, preferred_element_type=F32)
    for jj in range(PEER_NK):
        for lg in range(tm // LANES):
            ls = slice(lg * LANES, (lg + 1) * LANES)
            row = lambda ref, h: jnp.broadcast_to(ref[h, jj:jj + 1, ls], (sub, LANES)).astype(BF16)
            cnt_rows = [row(cnt_ref, h) for h in range(PEER_HEADS)]
            a_rows = [row(a_ref, h) for h in range(PEER_HEADS)]
            for st in range(N_KEYS // sub):
                terms = []
                for h in range(PEER_HEADS):
                    rs = slice(h * N_KEYS + st * sub, h * N_KEYS + (st + 1) * sub)
                    terms.append(jnp.where(r2_ref[rs, ls] < cnt_rows[h], a_rows[h] * b_ref[rs, ls], 0.0))
                while len(terms) > 1:
                    terms = [terms[i] + terms[i + 1] for i in range(0, len(terms), 2)]
                hs = slice(jj * N_KEYS + st * sub, jj * N_KEYS + (st + 1) * sub)
                hv = h_ref[hs, ls]
                act = 0.5 * hv * (1.0 + lax.erf(hv * (2.0 ** -0.5)))
                p_ref[hs, ls] = terms[0] * act.astype(BF16)
    o_ref[...] += jnp.dot(vt_ref[...], p_ref[...], preferred_element_type=F32)


def _peer_experts(xn, u, vt_chunks, a, cnt, b, r2, tm=1024):
    t = xn.shape[0]
    te = PEER_NK * N_KEYS
    once = pl.Buffered(1)
    row_spec = pl.BlockSpec((PEER_HEADS, PEER_NK, tm), lambda i, j: (0, j, i))
    all_spec = pl.BlockSpec((PEER_HEADS * N_KEYS, tm), lambda i, j: (0, i), pipeline_mode=once)
    return pl.pallas_call(
        _peer_expert_kernel,
        grid=(t // tm, N_EXPERTS // te),
        in_specs=[pl.BlockSpec((tm, D_MODEL), lambda i, j: (i, 0), pipeline_mode=once),
                  pl.BlockSpec((te, D_MODEL), lambda i, j: (j, 0)),
                  pl.BlockSpec((None, D_MODEL, te), lambda i, j: (j, 0, 0)),
                  row_spec, row_spec, all_spec, all_spec],
        out_specs=pl.BlockSpec((D_MODEL, tm), lambda i, j: (0, i)),
        out_shape=jax.ShapeDtypeStruct((D_MODEL, t), F32),
        scratch_shapes=[pltpu.VMEM((te, tm), F32),
                        pltpu.VMEM((te, tm), BF16),
                        pltpu.VMEM((PEER_HEADS * N_KEYS, tm), BF16),
                        pltpu.VMEM((PEER_HEADS * N_KEYS, tm), BF16)],
        compiler_params=pltpu.CompilerParams(dimension_semantics=("parallel", "arbitrary"),
                                             vmem_limit_bytes=PEER_VMEM_LIMIT),
        name="peer_experts",
    )(xn, u, vt_chunks, a, cnt, b, r2)


def _rope_tables(seq):
    half = HEAD_DIM // 2
    inv_freq = ROPE_THETA ** (-jnp.arange(half, dtype=F32) / half)
    ang = jnp.arange(seq).astype(F32)[:, None] * inv_freq[None, :]
    cos, sin = jnp.cos(ang), jnp.sin(ang)
    return jnp.concatenate([cos, cos], axis=-1), jnp.concatenate([-sin, sin], axis=-1)


def kernel(x, mix_norm, w_in, b_forget, b_gate, q_norm, k_norm, w_branch_a, w_branch_b, w_out,
           ffn_norm, w_query, sub_keys, expert_u, expert_v):
    batch, seq, dm = x.shape
    t = batch * seq
    depth = w_in.shape[0]
    cos, sin = _rope_tables(seq)
    hd = HEAD_DIM
    nd, nh = DIL_HEADS, N_HEADS
    xf = x.reshape(t, dm)
    xn = _rmsnorm(xf, mix_norm[0])
    for l in range(depth):
        wl = w_in[l]
        wq, wk, wv = wl[:, :nh * hd], wl[:, nh * hd:2 * nh * hd], wl[:, 2 * nh * hd:QKV_COLS]
        w_rope = jnp.concatenate([wq[:, :nd * hd], wk[:, :nd * hd]], axis=1).astype(BF16)
        w_fox = jnp.concatenate([wq[:, nd * hd:], wk[:, nd * hd:]], axis=1).astype(BF16)
        g_rope = jnp.concatenate([q_norm[l, :nd].reshape(1, -1), k_norm[l, :nd].reshape(1, -1)], axis=1)
        g_fox = jnp.concatenate([q_norm[l, nd:].reshape(1, -1), k_norm[l, nd:].reshape(1, -1)], axis=1)
        w_f = jnp.pad(wl[:, QKV_COLS:QKV_COLS + FOX_HEADS], ((0, 0), (0, LANES - FOX_HEADS))).astype(BF16)
        b_f = jnp.pad(b_forget[l], (0, LANES - FOX_HEADS)).reshape(1, LANES)
        w_g = wl[:, QKV_COLS + FOX_HEADS:].astype(BF16)

        qk_a = _proj(xn, w_rope, g_rope, mode="qk_rope", out_dtype=BF16, seq=seq, cos=cos, sin=sin)
        qk_b = _proj(xn, w_fox, g_fox, mode="qk", out_dtype=BF16, seq=seq, cos=cos, sin=sin)
        zeros = lambda n: jnp.zeros((1, n), F32)
        v_a = _proj(xn, wv[:, :nd * hd].astype(BF16), zeros(nd * hd), mode="none", out_dtype=BF16, seq=seq)
        v_b = _proj(xn, wv[:, nd * hd:].astype(BF16), zeros(B_OUT), mode="none", out_dtype=BF16, seq=seq)
        gate = _proj(xn, w_g, b_gate[l].reshape(1, -1), mode="sigmoid", out_dtype=F32, seq=seq)
        log_f = _proj(xn, w_f, b_f, mode="log_sigmoid", out_dtype=F32, seq=seq)

        outs, lses = [], []
        for gi, (_, dilation) in enumerate(DIL_GROUPS):
            o_g, l_g = _dilated(qk_a, v_a, gi, dilation, batch, seq)
            outs.append(o_g)
            lses.append(l_g)
        d_nat, d_t = _decay(log_f, batch, seq)
        o_b = _fox(qk_b, v_b, d_nat, d_t, batch, seq)

        xf, xn2 = _mix(outs, lses, o_b, gate, xf, w_branch_a[l].astype(BF16), w_branch_b[l].astype(BF16),
                       w_out[l].astype(BF16), ffn_norm[l])

        keys = sub_keys[l].reshape(2 * PEER_HEADS, N_KEYS, N_KEYS).astype(BF16)
        a, cnt, b, r2 = _peer_select(xn2, w_query[l].astype(BF16), keys)
        te = PEER_NK * N_KEYS
        vt_chunks = expert_v[l].reshape(N_EXPERTS // te, te, dm).transpose(0, 2, 1).astype(BF16)
        peer_t = _peer_experts(xn2, expert_u[l].astype(BF16), vt_chunks, a, cnt, b, r2)
        if l + 1 < depth:
            xf, xn = _add_transposed(xf, peer_t, mix_norm[l + 1])
        else:
            xf = _add_transposed(xf, peer_t)
    return xf.reshape(batch, seq, dm)
```
